```python
import math
import jax
import jax.numpy as jnp
from jax import lax
import numpy as np

D_MODEL = 1024
BATCH = 16
SEQ = 4096
DEPTH = 4
DEC_BATCH = 16
DEC_SEQ = 16
PAST_LEN = 2048

CHUNK = 64
Q_BLOCK = 128
H_A = 4
DH_A = 64
A_WIDTH = H_A * 2 * DH_A
H_M = 4
DH_M = 128
M_WIDTH = H_M * DH_M
CONV_W = 4
D_FF = 2816
N_MOD = 9
EPS = 1e-6
ALIBI_SLOPES = tuple(2.0 ** (-8.0 * (i + 1) / H_A) for i in range(H_A))

OFF_AQ = 0
OFF_AK = OFF_AQ + A_WIDTH
OFF_AV = OFF_AK + A_WIDTH
OFF_MQ = OFF_AV + A_WIDTH
OFF_MK = OFF_MQ + M_WIDTH
OFF_MV = OFF_MK + M_WIDTH
OFF_MO = OFF_MV + M_WIDTH
OFF_MI = OFF_MO + M_WIDTH
OFF_MF = OFF_MI + H_M
OFF_G = OFF_MF + H_M
IN_WIDTH = OFF_G + 2 * D_MODEL

kernel_name = "streaming_diffattn_mlstm_macaron_step"


def _rmsnorm(x, g):
    xf = x.astype(jnp.float32)
    y = xf * lax.rsqrt(jnp.mean(xf * xf, axis=-1, keepdims=True) + EPS)
    return (y * g.astype(jnp.float32)).astype(x.dtype)


def _swiglu(h, w1, w3, w2):
    return (jax.nn.silu(h @ w1) * (h @ w3)) @ w2


def _causal_conv(u, buf, w, b):
    t = u.shape[1]
    full = jnp.concatenate([buf.astype(u.dtype), u], axis=1)
    y = b
    for j in range(CONV_W):
        y = y + full[:, j:j + t] * w[j]
    return y, full[:, full.shape[1] - (CONV_W - 1):]


def _diff_attn_scores(q, k, v, q_pos, k_pos, mask, lmb):
    s = jnp.einsum("bqhcd,bkhcd->bhcqk", q, k).astype(jnp.float32) * (DH_A ** -0.5)
    slopes = jnp.asarray(ALIBI_SLOPES, jnp.float32)
    dist = jnp.abs(q_pos[:, None] - k_pos[None, :]).astype(jnp.float32)
    s = s - (slopes[:, None, None] * dist)[None, :, None]
    if mask is not None:
        s = jnp.where(mask, s, -jnp.inf)
    p = jax.nn.softmax(s, axis=-1)
    a = p[:, :, 0] - lmb * p[:, :, 1]
    return jnp.einsum("bhqk,bkhe->bqhe", a.astype(v.dtype), v)


def _diff_attn_prompt(q, k, v, lmb):
    bsz, s = q.shape[:2]
    nb = s // Q_BLOCK
    k_pos = jnp.arange(s)
    q_blocks = jnp.moveaxis(q.reshape(bsz, nb, Q_BLOCK, H_A, 2, DH_A), 1, 0)

    def block(args):
        i, qi = args
        q_pos = i * Q_BLOCK + jnp.arange(Q_BLOCK)
        mask = (k_pos[None, :] // CHUNK) <= (q_pos[:, None] // CHUNK)
        return _diff_attn_scores(qi, k, v, q_pos, k_pos, mask, lmb)

    o = lax.map(block, (jnp.arange(nb), q_blocks))
    return jnp.moveaxis(o, 0, 1).reshape(bsz, s, H_A, 2 * DH_A)


def _diff_attn_sample(q, k, v, past_k, past_v, lmb):
    bsz, t = q.shape[:2]
    p = past_k.shape[1]
    k_all = jnp.concatenate([past_k.astype(k.dtype).reshape(bsz, p, H_A, 2, DH_A), k], axis=1)
    v_all = jnp.concatenate([past_v.astype(v.dtype), v], axis=1)
    k_pos = jnp.arange(p + t)
    q_pos = p + jnp.arange(t)
    return _diff_attn_scores(q, k_all, v_all, q_pos, k_pos, None, lmb)


def _mlstm(q, k, v, log_i, log_f, c0, n0, m0, blk):
    bsz, t = q.shape[:2]
    nc = t // blk

    def chunks(a):
        a = a.astype(jnp.float32).reshape(bsz, nc, blk, *a.shape[2:])
        return jnp.moveaxis(jnp.moveaxis(a, 1, 0), 2, 3)

    causal = jnp.tril(jnp.ones((blk, blk), dtype=bool))

    def step(carry, xs):
        c, n, m = carry
        qc, kc, vc, li, lf = xs
        bcum = jnp.cumsum(lf, axis=-1)
        dmat = jnp.where(causal, bcum[..., :, None] - bcum[..., None, :] + li[..., None, :], -jnp.inf)
        inter = bcum + m[..., None]
        mt = jnp.maximum(inter, jnp.max(dmat, axis=-1))
        sw = jnp.einsum("bhtd,bhsd->bhts", qc, kc) * jnp.exp(dmat - mt[..., None])
        w_inter = jnp.exp(inter - mt)
        num = jnp.einsum("bhts,bhse->bhte", sw, vc) + w_inter[..., None] * jnp.einsum("bhed,bhtd->bhte", c, qc)
        den = jnp.sum(sw, axis=-1) + w_inter * jnp.einsum("bhd,bhtd->bht", n, qc)
        h = num / jnp.maximum(jnp.abs(den), jnp.exp(-mt))[..., None]
        b_last = bcum[..., -1]
        g = b_last[..., None] - bcum + li
        m_new = jnp.maximum(b_last + m, jnp.max(g, axis=-1))
        ws = jnp.exp(g - m_new[..., None])
        wc = jnp.exp(b_last + m - m_new)
        c_new = wc[..., None, None] * c + jnp.einsum("bhs,bhse,bhsd->bhed", ws, vc, kc)
        n_new = wc[..., None] * n + jnp.einsum("bhs,bhsd->bhd", ws, kc)
        return (c_new, n_new, m_new), h

    carry0 = (c0.astype(jnp.float32), n0.astype(jnp.float32), m0.astype(jnp.float32))
    (c_f, n_f, m_f), hs = lax.scan(step, carry0, (chunks(q), chunks(k), chunks(v), chunks(log_i), chunks(log_f)))
    hs = jnp.moveaxis(jnp.moveaxis(hs, 0, 1), 2, 3).reshape(bsz, t, H_M, DH_M)
    return hs, (c_f, n_f, m_f)


def _token_mixer(h, l, past, w_in, b_if, conv_w, conv_b, lam, norm_a, norm_m, w_pa, w_pb, w_out):
    bsz, t, _ = h.shape
    z = h @ w_in[l]
    q = z[..., OFF_AQ:OFF_AK].reshape(bsz, t, H_A, 2, DH_A)
    k = z[..., OFF_AK:OFF_AV].reshape(bsz, t, H_A, 2, DH_A)
    v = z[..., OFF_AV:OFF_MQ].reshape(bsz, t, H_A, 2 * DH_A)
    lam_init = 0.8 - 0.6 * math.exp(-0.3 * l)
    lp = lam[l].astype(jnp.float32)
    lmb = jnp.exp(jnp.sum(lp[0] * lp[1])) - jnp.exp(jnp.sum(lp[2] * lp[3])) + lam_init
    if past is None:
        o_a = _diff_attn_prompt(q, k, v, lmb)
        c0 = jnp.zeros((bsz, H_M, DH_M, DH_M), jnp.float32)
        n0 = jnp.zeros((bsz, H_M, DH_M), jnp.float32)
        m0 = jnp.zeros((bsz, H_M), jnp.float32)
        conv0 = jnp.zeros((bsz, CONV_W - 1, 2 * M_WIDTH), h.dtype)
        blk = CHUNK
    else:
        past_k, past_v, c0, n0, m0, conv0 = past
        o_a = _diff_attn_sample(q, k, v, past_k, past_v, lmb)
        blk = t
    o_a = (_rmsnorm(o_a, norm_a[l]) * (1.0 - lam_init)).reshape(bsz, t, A_WIDTH)
    qk, conv_new = _causal_conv(z[..., OFF_MQ:OFF_MV], conv0, conv_w[l], conv_b[l])
    qk = jax.nn.silu(qk)
    qm = qk[..., :M_WIDTH].reshape(bsz, t, H_M, DH_M)
    km = qk[..., M_WIDTH:].reshape(bsz, t, H_M, DH_M) * (DH_M ** -0.5)
    vm = z[..., OFF_MV:OFF_MO].reshape(bsz, t, H_M, DH_M)
    og = jax.nn.sigmoid(z[..., OFF_MO:OFF_MI]).reshape(bsz, t, H_M, DH_M)
    gates = z[..., OFF_MI:OFF_G].astype(jnp.float32) + b_if[l].astype(jnp.float32)
    log_i = gates[..., :H_M]
    log_f = jax.nn.log_sigmoid(gates[..., H_M:])
    hm, (c_new, n_new, m_new) = _mlstm(qm, km, vm, log_i, log_f, c0, n0, m0, blk)
    o_m = _rmsnorm(og * hm.astype(h.dtype), norm_m[l]).reshape(bsz, t, M_WIDTH)
    g = jax.nn.sigmoid(z[..., OFF_G:])
    merged = g[..., :D_MODEL] * (o_a @ w_pa[l]) + g[..., D_MODEL:] * (o_m @ w_pb[l])
    y = merged @ w_out[l]
    k_rows = k.reshape(bsz, t, H_A, 2 * DH_A)
    return y, (k_rows, v, c_new, n_new, m_new, conv_new)


def _trunk(x, c, past, w_ada, b_ada, norm_g, w_ffn1, w_ffn3, w_ffn2, w_in, b_if, conv_w, conv_b,
           lam, norm_a, norm_m, w_pa, w_pb, w_out):
    bsz = x.shape[0]
    new = ([], [], [], [], [], [])
    for l in range(DEPTH):
        mod = (jax.nn.silu(c) @ w_ada[l] + b_ada[l]).reshape(bsz, N_MOD, 1, D_MODEL)
        sh1, sc1, g1, sh2, sc2, g2, sh3, sc3, g3 = [mod[:, i] for i in range(N_MOD)]
        h = _rmsnorm(x, norm_g[l, 0]) * (1.0 + sc1) + sh1
        x = x + 0.5 * g1 * _rmsnorm(_swiglu(h, w_ffn1[l, 0], w_ffn3[l, 0], w_ffn2[l, 0]), norm_g[l, 1])
        h = _rmsnorm(x, norm_g[l, 2]) * (1.0 + sc2) + sh2
        layer_past = None if past is None else tuple(a[l] for a in past)
        y, states = _token_mixer(h, l, layer_past, w_in, b_if, conv_w, conv_b, lam, norm_a, norm_m,
                                 w_pa, w_pb, w_out)
        x = x + g2 * _rmsnorm(y, norm_g[l, 3])
        h = _rmsnorm(x, norm_g[l, 4]) * (1.0 + sc3) + sh3
        x = x + 0.5 * g3 * _rmsnorm(_swiglu(h, w_ffn1[l, 1], w_ffn3[l, 1], w_ffn2[l, 1]), norm_g[l, 5])
        for lst, s in zip(new, states):
            lst.append(s)
    return x, [jnp.stack(s) for s in new]


def setup_inputs(seed: int = 0) -> dict:
    key = jax.random.key(seed)
    ks = jax.random.split(key, 32)
    f32 = jnp.float32

    def nrm(k, shape, scale):
        return scale * jax.random.normal(k, shape, f32)

    h_idx = jnp.arange(H_M, dtype=f32)
    f_bias = jnp.broadcast_to(3.0 + 3.0 * h_idx / (H_M - 1), (DEPTH, H_M)) + nrm(ks[20], (DEPTH, H_M), 0.1)
    b_if = jnp.concatenate([nrm(ks[19], (DEPTH, H_M), 0.1), f_bias], axis=-1)
    return {
        "x_prompt": nrm(ks[0], (BATCH, SEQ, D_MODEL), 1.0),
        "x_sample": nrm(ks[1], (DEC_BATCH, DEC_SEQ, D_MODEL), 1.0),
        "c_prompt": nrm(ks[2], (BATCH, D_MODEL), 1.0),
        "c_sample": nrm(ks[3], (DEC_BATCH, D_MODEL), 1.0),
        "cache_k": nrm(ks[4], (DEPTH, DEC_BATCH, PAST_LEN, H_A, 2 * DH_A), 1.0),
        "cache_v": nrm(ks[5], (DEPTH, DEC_BATCH, PAST_LEN, H_A, 2 * DH_A), 1.0),
        "state_c": nrm(ks[6], (DEPTH, DEC_BATCH, H_M, DH_M, DH_M), 0.1),
        "state_n": nrm(ks[7], (DEPTH, DEC_BATCH, H_M, DH_M), 0.1),
        "state_m": nrm(ks[8], (DEPTH, DEC_BATCH, H_M), 0.5),
        "state_conv": nrm(ks[9], (DEPTH, DEC_BATCH, CONV_W - 1, 2 * M_WIDTH), 1.0),
        "w_ada": nrm(ks[10], (DEPTH, D_MODEL, N_MOD * D_MODEL), 0.5 * D_MODEL ** -0.5),
        "b_ada": nrm(ks[11], (DEPTH, N_MOD * D_MODEL), 0.01),
        "norm_g": 1.0 + nrm(ks[12], (DEPTH, 6, D_MODEL), 0.05),
        "w_ffn1": nrm(ks[13], (DEPTH, 2, D_MODEL, D_FF), D_MODEL ** -0.5),
        "w_ffn3": nrm(ks[14], (DEPTH, 2, D_MODEL, D_FF), D_MODEL ** -0.5),
        "w_ffn2": nrm(ks[15], (DEPTH, 2, D_FF, D_MODEL), D_FF ** -0.5),
        "w_in": nrm(ks[16], (DEPTH, D_MODEL, IN_WIDTH), D_MODEL ** -0.5),
        "b_if": b_if,
        "conv_w": nrm(ks[17], (DEPTH, CONV_W, 2 * M_WIDTH), CONV_W ** -0.5),
        "conv_b": nrm(ks[18], (DEPTH, 2 * M_WIDTH), 0.01),
        "lam": nrm(ks[21], (DEPTH, 4, DH_A), 0.1),
        "norm_a": 1.0 + nrm(ks[22], (DEPTH, 2 * DH_A), 0.05),
        "norm_m": 1.0 + nrm(ks[23], (DEPTH, DH_M), 0.05),
        "w_pa": nrm(ks[24], (DEPTH, A_WIDTH, D_MODEL), A_WIDTH ** -0.5),
        "w_pb": nrm(ks[25], (DEPTH, M_WIDTH, D_MODEL), M_WIDTH ** -0.5),
        "w_out": nrm(ks[26], (DEPTH, D_MODEL, D_MODEL), D_MODEL ** -0.5),
    }


def reference(x_prompt, x_sample, c_prompt, c_sample, cache_k, cache_v, state_c, state_n, state_m,
              state_conv, w_ada, b_ada, norm_g, w_ffn1, w_ffn3, w_ffn2, w_in, b_if, conv_w, conv_b,
              lam, norm_a, norm_m, w_pa, w_pb, w_out):
    y_prompt, (pk, pv, pc, pn, pm, pconv) = _trunk(
        x_prompt, c_prompt, None, w_ada, b_ada, norm_g, w_ffn1, w_ffn3, w_ffn2, w_in, b_if,
        conv_w, conv_b, lam, norm_a, norm_m, w_pa, w_pb, w_out)
    y_sample, (sk, sv, sc, sn, sm, sconv) = _trunk(
        x_sample, c_sample, (cache_k, cache_v, state_c, state_n, state_m, state_conv),
        w_ada, b_ada, norm_g, w_ffn1, w_ffn3, w_ffn2, w_in, b_if,
        conv_w, conv_b, lam, norm_a, norm_m, w_pa, w_pb, w_out)
    return (y_prompt, y_sample, pk, pv, pc, pn, pm, pconv, sk, sv, sc, sn, sm, sconv)
```

```python
import functools
import math

import numpy as np
import jax
import jax.numpy as jnp
from jax import lax
from jax.experimental import pallas as pl
from jax.experimental.pallas import tpu as pltpu

D_MODEL = 1024
CHUNK = 64
H_A = 4
DH_A = 64
A_WIDTH = H_A * 2 * DH_A
H_M = 4
DH_M = 128
M_WIDTH = H_M * DH_M
CONV_W = 4
D_FF = 2816
N_MOD = 9
EPS = 1e-6
ALIBI_SLOPES = tuple(2.0 ** (-8.0 * (i + 1) / H_A) for i in range(H_A))
DEPTH = 4

OFF_AQ = 0
OFF_AK = OFF_AQ + A_WIDTH
OFF_AV = OFF_AK + A_WIDTH
OFF_MQ = OFF_AV + A_WIDTH
OFF_MK = OFF_MQ + M_WIDTH
OFF_MV = OFF_MK + M_WIDTH
OFF_MO = OFF_MV + M_WIDTH
OFF_MI = OFF_MO + M_WIDTH
OFF_MF = OFF_MI + H_M
OFF_G = OFF_MF + H_M

LANES = 128
SUBLANES = 8
GATE_ROWS = 16
VMEM_LIMIT = 56 * 1024 * 1024
NEG = -1e30

BF16 = jnp.bfloat16
F32 = jnp.float32


def _dot(a, b):
    return jnp.dot(a, b, preferred_element_type=F32)


def _dot_nt(a, b):
    return lax.dot_general(a, b, (((1,), (1,)), ((), ())), preferred_element_type=F32)


def _dot_tn(a, b):
    return lax.dot_general(a, b, (((0,), (0,)), ((), ())), preferred_element_type=F32)


def _rms(x):
    return x * lax.rsqrt(jnp.mean(x * x, axis=-1, keepdims=True) + EPS)


def _mod_norm(x, g, sc, sh):
    return (_rms(x) * g) * (1.0 + sc) + sh


def _sigmoid(x):
    return 1.0 / (1.0 + jnp.exp(-x))


def _log_sigmoid(x):
    return jnp.minimum(x, 0.0) - jnp.log(1.0 + jnp.exp(-jnp.abs(x)))


def _params(*sem):
    return pltpu.CompilerParams(dimension_semantics=sem, vmem_limit_bytes=VMEM_LIMIT)


def _ada_kernel(c_ref, w_ref, b_ref, o_ref):
    c = c_ref[...]
    a = (c * _sigmoid(c)).astype(BF16)
    o_ref[0] = _dot(a, w_ref[0].astype(BF16)) + b_ref[0]


def _ada(c_all, w_ada, b_ada):
    depth, _, width = w_ada.shape
    rows = c_all.shape[0]
    tn = width // 8
    return pl.pallas_call(
        _ada_kernel,
        grid=(depth, width // tn),
        in_specs=[
            pl.BlockSpec((rows, D_MODEL), lambda l, j: (0, 0)),
            pl.BlockSpec((1, D_MODEL, tn), lambda l, j: (l, 0, j)),
            pl.BlockSpec((1, 1, tn), lambda l, j: (l, 0, j)),
        ],
        out_specs=pl.BlockSpec((1, rows, tn), lambda l, j: (l, 0, j)),
        out_shape=jax.ShapeDtypeStruct((depth, rows, width), F32),
        compiler_params=_params("arbitrary", "arbitrary"),
        name="ada_mod",
    )(c_all, w_ada, b_ada.reshape(depth, 1, width))


def _ffn_kernel(x_ref, sh_ref, sc_ref, gt_ref, gpre_ref, gpost_ref, w1_ref, w3_ref, w2_ref,
                o_ref, h_scr, acc_scr):
    j = pl.program_id(1)

    @pl.when(j == 0)
    def _():
        h = _mod_norm(x_ref[...], gpre_ref[...], sc_ref[0], sh_ref[0])
        h_scr[...] = h.astype(BF16)
        acc_scr[...] = jnp.zeros_like(acc_scr)

    h = h_scr[...]
    a = _dot(h, w1_ref[...])
    b = _dot(h, w3_ref[...])
    u = (a * _sigmoid(a)) * b
    acc_scr[...] += _dot(u.astype(BF16), w2_ref[...])

    @pl.when(j == pl.num_programs(1) - 1)
    def _():
        y = _rms(acc_scr[...]) * gpost_ref[...]
        o_ref[...] = x_ref[...] + 0.5 * gt_ref[0] * y


def _ffn(x, sh, sc, gt, g_pre, g_post, w1, w3, w2, *, tm, tf, tpg):
    n = x.shape[0]
    r = sh.shape[1]
    mod_spec = pl.BlockSpec((1, r, D_MODEL), lambda i, j: (i // tpg, 0, 0))
    vec_spec = pl.BlockSpec((1, D_MODEL), lambda i, j: (0, 0))
    return pl.pallas_call(
        _ffn_kernel,
        grid=(n // tm, D_FF // tf),
        in_specs=[
            pl.BlockSpec((tm, D_MODEL), lambda i, j: (i, 0)),
            mod_spec, mod_spec, mod_spec, vec_spec, vec_spec,
            pl.BlockSpec((D_MODEL, tf), lambda i, j: (0, j)),
            pl.BlockSpec((D_MODEL, tf), lambda i, j: (0, j)),
            pl.BlockSpec((tf, D_MODEL), lambda i, j: (j, 0)),
        ],
        out_specs=pl.BlockSpec((tm, D_MODEL), lambda i, j: (i, 0)),
        out_shape=jax.ShapeDtypeStruct((n, D_MODEL), F32),
        scratch_shapes=[pltpu.VMEM((tm, D_MODEL), BF16), pltpu.VMEM((tm, D_MODEL), F32)],
        compiler_params=_params("arbitrary", "arbitrary"),
        name="ffn_half_step",
    )(x, sh, sc, gt, g_pre, g_post, w1, w3, w2)


def _chunk_cumsum_cols(lf, chunk):
    t = lf.shape[0]
    if t == chunk:
        out = jnp.zeros_like(lf)
        r = lax.broadcasted_iota(jnp.int32, (t, 1), 0)
        for s in range(t):
            out = out + jnp.where(r >= s, 1.0, 0.0) * lf[s:s + 1, :]
        return out
    shift = int(math.log2(chunk))
    r = lax.broadcasted_iota(jnp.int32, (t, t), 0)
    c = lax.broadcasted_iota(jnp.int32, (t, t), 1)
    tri = jnp.where((jnp.right_shift(r, shift) == jnp.right_shift(c, shift)) & (c <= r), 1.0, 0.0)
    return jnp.dot(tri, lf, preferred_element_type=F32, precision=lax.Precision.HIGHEST)


def _chunk_cumsum_rows(lf_t, chunk):
    t = lf_t.shape[1]
    if t == chunk:
        out = jnp.zeros_like(lf_t)
        c = lax.broadcasted_iota(jnp.int32, (1, t), 1)
        for s in range(t):
            out = out + lf_t[:, s:s + 1] * jnp.where(c >= s, 1.0, 0.0)
        return out
    shift = int(math.log2(chunk))
    r = lax.broadcasted_iota(jnp.int32, (t, t), 0)
    c = lax.broadcasted_iota(jnp.int32, (t, t), 1)
    tri = jnp.where((jnp.right_shift(r, shift) == jnp.right_shift(c, shift)) & (r <= c), 1.0, 0.0)
    return jnp.dot(lf_t, tri, preferred_element_type=F32, precision=lax.Precision.HIGHEST)


def _inproj_kernel(x_ref, sh_ref, sc_ref, gpre_ref, w_ref, wg_ref, wgt_ref, bgr_ref, bgc_ref,
                   cw_ref, cb_ref, conv0_ref,
                   q_ref, k_ref, v_ref, kb_ref, vb_ref, mq_ref, mk_ref, mv_ref, og_ref,
                   gc_ref, gr_ref, tail_ref, ubuf, *, tm, tps, chunk):
    i = pl.program_id(0)
    h = _mod_norm(x_ref[...], gpre_ref[...], sc_ref[0], sh_ref[0]).astype(BF16)

    q_ref[...] = (_dot(h, w_ref[:, OFF_AQ:OFF_AK]) * (DH_A ** -0.5)).astype(BF16)
    zk = _dot(h, w_ref[:, OFF_AK:OFF_AV])
    k_ref[...] = zk
    kb_ref[...] = zk.astype(BF16)
    zv = _dot(h, w_ref[:, OFF_AV:OFF_MQ])
    v_ref[...] = zv
    vb_ref[...] = zv.astype(BF16)

    u = _dot(h, w_ref[:, OFF_MQ:OFF_MV])
    first = (i % tps) == 0

    @pl.when(first)
    def _():
        ubuf[0:SUBLANES, :] = conv0_ref[0]

    @pl.when(jnp.logical_not(first))
    def _():
        ubuf[0:SUBLANES, :] = ubuf[tm:tm + SUBLANES, :]

    ubuf[SUBLANES:SUBLANES + tm, :] = u
    y = cb_ref[...] + u * cw_ref[CONV_W - 1:CONV_W, :]
    for j in range(1, CONV_W):
        y = y + ubuf[SUBLANES - j:SUBLANES - j + tm, :] * cw_ref[CONV_W - 1 - j:CONV_W - j, :]
    y = y * _sigmoid(y)
    mq_ref[...] = y[:, :M_WIDTH].astype(BF16)
    mk_ref[...] = (y[:, M_WIDTH:] * (DH_M ** -0.5)).astype(BF16)
    tail_ref[0] = ubuf[tm:tm + SUBLANES, :]

    mv_ref[...] = _dot(h, w_ref[:, OFF_MV:OFF_MO]).astype(BF16)
    og_ref[...] = _sigmoid(_dot(h, w_ref[:, OFF_MO:OFF_MI]))

    g = _dot(h, wg_ref[...]) + bgr_ref[...]
    lf = _log_sigmoid(g)
    col = lax.broadcasted_iota(jnp.int32, g.shape, 1)
    gc_ref[...] = jnp.where(col < H_M, g, jnp.where(col < 2 * H_M, lf, _chunk_cumsum_cols(lf, chunk)))
    gt = _dot_nt(wgt_ref[...], h) + bgc_ref[...]
    lft = _log_sigmoid(gt)
    row = lax.broadcasted_iota(jnp.int32, gt.shape, 0)
    gr_ref[0] = jnp.where(row < H_M, gt, jnp.where(row < 2 * H_M, lft, _chunk_cumsum_rows(lft, chunk)))


def _inproj(x, sh, sc, g_pre, w_main, w_gate, w_gate_t, b_gate_row, b_gate_col, conv_w, conv_b,
            conv0, *, tm, tps, chunk):
    n = x.shape[0]
    nb = conv0.shape[0]
    nt = n // tm
    row = lambda i: (i, 0)
    const2 = lambda i: (0, 0)
    mod_spec = pl.BlockSpec((1, 1, D_MODEL), lambda i: (i // tps, 0, 0))
    tok = lambda w, dt: jax.ShapeDtypeStruct((n, w), dt)
    out_shape = (
        tok(A_WIDTH, BF16), tok(A_WIDTH, F32), tok(A_WIDTH, F32), tok(A_WIDTH, BF16), tok(A_WIDTH, BF16),
        tok(M_WIDTH, BF16), tok(M_WIDTH, BF16), tok(M_WIDTH, BF16), tok(M_WIDTH, F32),
        tok(LANES, F32),
        jax.ShapeDtypeStruct((nt, GATE_ROWS, tm), F32),
        jax.ShapeDtypeStruct((nb, SUBLANES, 2 * M_WIDTH), F32),
    )
    out_specs = (
        [pl.BlockSpec((tm, A_WIDTH), row)] * 5 + [pl.BlockSpec((tm, M_WIDTH), row)] * 4
        + [pl.BlockSpec((tm, LANES), row),
           pl.BlockSpec((1, GATE_ROWS, tm), lambda i: (i, 0, 0)),
           pl.BlockSpec((1, SUBLANES, 2 * M_WIDTH), lambda i: (i // tps, 0, 0))]
    )
    return pl.pallas_call(
        functools.partial(_inproj_kernel, tm=tm, tps=tps, chunk=chunk),
        grid=(nt,),
        in_specs=[
            pl.BlockSpec((tm, D_MODEL), row),
            mod_spec, mod_spec,
            pl.BlockSpec((1, D_MODEL), const2),
            pl.BlockSpec(w_main.shape, const2),
            pl.BlockSpec(w_gate.shape, const2),
            pl.BlockSpec(w_gate_t.shape, const2),
            pl.BlockSpec(b_gate_row.shape, const2),
            pl.BlockSpec(b_gate_col.shape, const2),
            pl.BlockSpec(conv_w.shape, const2),
            pl.BlockSpec(conv_b.shape, const2),
            pl.BlockSpec((1, SUBLANES, 2 * M_WIDTH), lambda i: (i // tps, 0, 0)),
        ],
        out_specs=out_specs,
        out_shape=out_shape,
        scratch_shapes=[pltpu.VMEM((tm + 2 * SUBLANES, 2 * M_WIDTH), F32)],
        compiler_params=_params("arbitrary"),
        name="mixer_in_proj",
    )(x, sh, sc, g_pre, w_main, w_gate, w_gate_t, b_gate_row, b_gate_col, conv_w, conv_b, conv0)


def _lambda_scalar(lam_ref, lam_init):
    lp = lam_ref[...]
    a = jnp.sum(lp[0:1, :] * lp[1:2, :], axis=-1, keepdims=True)
    b = jnp.sum(lp[2:3, :] * lp[3:4, :], axis=-1, keepdims=True)
    return jnp.exp(a) - jnp.exp(b) + lam_init


def _head_norm(o, g, scale):
    return (_rms(o) * g) * scale


def _attn_prompt_kernel(qi_tab, ki_tab, q_ref, k_ref, v_ref, diag_ref, slope_ref, lam_ref, na_ref,
                        o_ref, m_scr, l_scr, acc_scr, *, t, lam_init):
    step = pl.program_id(2)
    qi = qi_tab[step]
    ki = ki_tab[step]

    @pl.when(ki == 0)
    def _():
        m_scr[...] = jnp.full_like(m_scr, NEG)
        l_scr[...] = jnp.zeros_like(l_scr)
        acc_scr[...] = jnp.zeros_like(acc_scr)

    q = q_ref[...]
    k = k_ref[...]
    v = v_ref[...]

    def update(bias):
        for c in range(2):
            s = _dot_nt(q[:, c * DH_A:(c + 1) * DH_A], k[:, c * DH_A:(c + 1) * DH_A]) + bias
            m_old = m_scr[c]
            m_new = jnp.maximum(m_old, jnp.max(s, axis=-1, keepdims=True))
            alpha = jnp.exp(m_old - m_new)
            p = jnp.exp(s - m_new)
            l_scr[c] = alpha * l_scr[c] + jnp.sum(p, axis=-1, keepdims=True)
            acc_scr[c] = alpha * acc_scr[c] + _dot(p.astype(BF16), v)
            m_scr[c] = m_new

    @pl.when(ki < qi)
    def _():
        cpos = lax.broadcasted_iota(jnp.int32, (1, t), 1) + (ki - qi) * t
        update(slope_ref[0] * cpos.astype(F32))

    @pl.when(ki == qi)
    def _():
        update(diag_ref[0])
        lmb = _lambda_scalar(lam_ref, lam_init)
        o = acc_scr[0] / l_scr[0] - lmb * (acc_scr[1] / l_scr[1])
        o_ref[...] = _head_norm(o, na_ref[...], 1.0 - lam_init).astype(BF16)


def _attn_tables(s, t):
    nq = s // t
    qi = np.array([a for a in range(nq) for _ in range(a + 1)], np.int32)
    ki = np.array([b for a in range(nq) for b in range(a + 1)], np.int32)
    r = np.arange(t)[:, None]
    c = np.arange(t)[None, :]
    slopes = np.array(ALIBI_SLOPES, np.float64)[:, None, None]
    diag = -slopes * np.abs(r - c) + slopes * r
    diag = np.where((c // CHUNK) <= (r // CHUNK), diag, NEG)
    slope_rows = np.broadcast_to(slopes, (H_A, 1, t))
    return qi, ki, diag.astype(np.float32), slope_rows.astype(np.float32)


def _attn_prompt(q, kb, vb, lam_l, norm_a_l, *, bsz, s, t, lam_init):
    n = q.shape[0]
    nq = s // t
    qi, ki, diag, slope_rows = _attn_tables(s, t)
    q_map = lambda b, h, st, qt, kt: (b * nq + qt[st], h)
    k_map = lambda b, h, st, qt, kt: (b * nq + kt[st], h)
    grid_spec = pltpu.PrefetchScalarGridSpec(
        num_scalar_prefetch=2,
        grid=(bsz, H_A, len(qi)),
        in_specs=[
            pl.BlockSpec((t, 2 * DH_A), q_map),
            pl.BlockSpec((t, 2 * DH_A), k_map),
            pl.BlockSpec((t, 2 * DH_A), k_map),
            pl.BlockSpec((1, t, t), lambda b, h, st, qt, kt: (h, 0, 0)),
            pl.BlockSpec((1, 1, t), lambda b, h, st, qt, kt: (h, 0, 0)),
            pl.BlockSpec((4, DH_A), lambda b, h, st, qt, kt: (0, 0)),
            pl.BlockSpec((1, 2 * DH_A), lambda b, h, st, qt, kt: (0, 0)),
        ],
        out_specs=pl.BlockSpec((t, 2 * DH_A), q_map),
        scratch_shapes=[pltpu.VMEM((2, t, 1), F32), pltpu.VMEM((2, t, 1), F32),
                        pltpu.VMEM((2, t, 2 * DH_A), F32)],
    )
    return pl.pallas_call(
        functools.partial(_attn_prompt_kernel, t=t, lam_init=lam_init),
        grid_spec=grid_spec,
        out_shape=jax.ShapeDtypeStruct((n, A_WIDTH), BF16),
        compiler_params=_params("arbitrary", "arbitrary", "arbitrary"),
        name="diff_attn_prompt",
    )(jnp.asarray(qi), jnp.asarray(ki), q, kb, vb, jnp.asarray(diag), jnp.asarray(slope_rows),
      lam_l, norm_a_l)


def _attn_sample_kernel(q_ref, kn_ref, vn_ref, kp_ref, vp_ref, slope_ref, lam_ref, na_ref, o_ref,
                        *, t, p, lam_init):
    q = q_ref[...]
    kn = kn_ref[...].astype(BF16)
    vn = vn_ref[...].astype(BF16)
    kp = kp_ref[0].astype(BF16)
    vp = vp_ref[0].astype(BF16)
    slope_p = slope_ref[0]
    slope_n = slope_p[:, :t]
    r = lax.broadcasted_iota(jnp.int32, (t, 1), 0)
    dist_p = (p + r - lax.broadcasted_iota(jnp.int32, (1, p), 1)).astype(F32)
    dist_n = jnp.abs(r - lax.broadcasted_iota(jnp.int32, (1, t), 1)).astype(F32)
    bias_p = -slope_p * dist_p
    bias_n = -slope_n * dist_n
    probs = []
    for c in range(2):
        qc = q[:, c * DH_A:(c + 1) * DH_A]
        sp = _dot_nt(qc, kp[:, c * DH_A:(c + 1) * DH_A]) + bias_p
        sn = _dot_nt(qc, kn[:, c * DH_A:(c + 1) * DH_A]) + bias_n
        m = jnp.maximum(jnp.max(sp, axis=-1, keepdims=True), jnp.max(sn, axis=-1, keepdims=True))
        ep = jnp.exp(sp - m)
        en = jnp.exp(sn - m)
        inv = 1.0 / (jnp.sum(ep, axis=-1, keepdims=True) + jnp.sum(en, axis=-1, keepdims=True))
        probs.append((ep * inv, en * inv))
    lmb = _lambda_scalar(lam_ref, lam_init)
    ap = probs[0][0] - lmb * probs[1][0]
    an = probs[0][1] - lmb * probs[1][1]
    o = _dot(ap.astype(BF16), vp) + _dot(an.astype(BF16), vn)
    o_ref[...] = _head_norm(o, na_ref[...], 1.0 - lam_init).astype(BF16)


def _attn_sample(q, k_new, v_new, past_k, past_v, lam_l, norm_a_l, *, bsz, t, lam_init):
    n = q.shape[0]
    p = past_k.shape[1]
    slope_rows = np.broadcast_to(np.array(ALIBI_SLOPES, np.float32)[:, None, None], (H_A, 1, p))
    blk = lambda b, h: (b, h)
    past = lambda b, h: (b, 0, h)
    return pl.pallas_call(
        functools.partial(_attn_sample_kernel, t=t, p=p, lam_init=lam_init),
        grid=(bsz, H_A),
        in_specs=[
            pl.BlockSpec((t, 2 * DH_A), blk),
            pl.BlockSpec((t, 2 * DH_A), blk),
            pl.BlockSpec((t, 2 * DH_A), blk),
            pl.BlockSpec((1, p, 2 * DH_A), past),
            pl.BlockSpec((1, p, 2 * DH_A), past),
            pl.BlockSpec((1, 1, p), lambda b, h: (h, 0, 0)),
            pl.BlockSpec((4, DH_A), lambda b, h: (0, 0)),
            pl.BlockSpec((1, 2 * DH_A), lambda b, h: (0, 0)),
        ],
        out_specs=pl.BlockSpec((t, 2 * DH_A), blk),
        out_shape=jax.ShapeDtypeStruct((n, A_WIDTH), BF16),
        compiler_params=_params("arbitrary", "arbitrary"),
        name="diff_attn_sample",
    )(q, k_new, v_new, past_k.reshape(bsz, p, A_WIDTH), past_v.reshape(bsz, p, A_WIDTH),
      jnp.asarray(slope_rows), lam_l, norm_a_l)


def _mlstm_kernel(mq_ref, mk_ref, mv_ref, og_ref, gc_ref, gr_ref, c0_ref, n0_ref, m0_ref, nm_ref,
                  om_ref, c_ref, n_ref, m_ref, *, chunk, n_chunks):
    @pl.when(pl.program_id(1) == 0)
    def _():
        c_ref[...] = c0_ref[...]
        n_ref[...] = n0_ref[...]
        m_ref[...] = m0_ref[...]

    rr = lax.broadcasted_iota(jnp.int32, (chunk, chunk), 0)
    cc = lax.broadcasted_iota(jnp.int32, (chunk, chunk), 1)
    causal = cc <= rr
    norm_m = nm_ref[...]

    def body(ci, carry):
        rows = pl.ds(pl.multiple_of(ci * chunk, chunk), chunk)
        gcol = gc_ref[rows, :]
        grow = gr_ref[ci]
        for h in range(H_M):
            hs = slice(h * DH_M, (h + 1) * DH_M)
            q = mq_ref[rows, hs]
            k = mk_ref[rows, hs]
            v = mv_ref[rows, hs]
            li_col = gcol[:, h:h + 1]
            bc_col = gcol[:, 2 * H_M + h:2 * H_M + h + 1]
            li_row = grow[h:h + 1, :]
            bc_row = grow[2 * H_M + h:2 * H_M + h + 1, :]
            cmat = c_ref[0, h]
            nvec = n_ref[0, h]
            m_s = m_ref[0, h][:, 0:1]

            dmat = jnp.where(causal, bc_col - bc_row + li_row, NEG)
            inter = bc_col + m_s
            mt = jnp.maximum(inter, jnp.max(dmat, axis=-1, keepdims=True))
            sw = _dot_nt(q, k) * jnp.exp(dmat - mt)
            w_inter = jnp.exp(inter - mt)
            qf = q.astype(F32)
            num = _dot(sw.astype(BF16), v) + w_inter * _dot_nt(q, cmat.astype(BF16))
            den = jnp.sum(sw, axis=-1, keepdims=True) + w_inter * jnp.sum(qf * nvec, axis=-1, keepdims=True)
            hm = num / jnp.maximum(jnp.abs(den), jnp.exp(-mt))

            b_last = bc_row[:, chunk - 1:chunk]
            g_row = b_last - bc_row + li_row
            m_new = jnp.maximum(b_last + m_s, jnp.max(g_row, axis=-1, keepdims=True))
            ws_col = jnp.exp(b_last - bc_col + li_col - m_new)
            wc = jnp.exp(b_last + m_s - m_new)
            vw = (v.astype(F32) * ws_col).astype(BF16)
            c_ref[0, h] = wc * cmat + _dot_tn(vw, k)
            n_ref[0, h] = wc * nvec + jnp.sum(ws_col * k.astype(F32), axis=0, keepdims=True)
            m_ref[0, h] = jnp.broadcast_to(m_new, (1, LANES))

            om_ref[rows, hs] = (_rms(og_ref[rows, hs] * hm) * norm_m).astype(BF16)
        return carry

    lax.fori_loop(0, n_chunks, body, 0)


def _mlstm(mq, mk, mv, og, gc, gr, c0, n0, m0, norm_m_l, *, bsz, seq, tt, chunk):
    n = mq.shape[0]
    steps = seq // tt
    n_chunks = tt // chunk
    tok = lambda b, j: (b * steps + j, 0)
    st4 = lambda b, j: (b, 0, 0, 0)
    return pl.pallas_call(
        functools.partial(_mlstm_kernel, chunk=chunk, n_chunks=n_chunks),
        grid=(bsz, steps),
        in_specs=[
            pl.BlockSpec((tt, M_WIDTH), tok),
            pl.BlockSpec((tt, M_WIDTH), tok),
            pl.BlockSpec((tt, M_WIDTH), tok),
            pl.BlockSpec((tt, M_WIDTH), tok),
            pl.BlockSpec((tt, LANES), tok),
            pl.BlockSpec((n_chunks, GATE_ROWS, chunk), lambda b, j: (b * steps + j, 0, 0)),
            pl.BlockSpec((1, H_M, DH_M, DH_M), st4),
            pl.BlockSpec((1, H_M, 1, DH_M), st4),
            pl.BlockSpec((1, H_M, 1, LANES), st4),
            pl.BlockSpec((1, DH_M), lambda b, j: (0, 0)),
        ],
        out_specs=(
            pl.BlockSpec((tt, M_WIDTH), tok),
            pl.BlockSpec((1, H_M, DH_M, DH_M), st4),
            pl.BlockSpec((1, H_M, 1, DH_M), st4),
            pl.BlockSpec((1, H_M, 1, LANES), st4),
        ),
        out_shape=(
            jax.ShapeDtypeStruct((n, M_WIDTH), BF16),
            jax.ShapeDtypeStruct((bsz, H_M, DH_M, DH_M), F32),
            jax.ShapeDtypeStruct((bsz, H_M, 1, DH_M), F32),
            jax.ShapeDtypeStruct((bsz, H_M, 1, LANES), F32),
        ),
        compiler_params=_params("arbitrary", "arbitrary"),
        name="mlstm_scan",
    )(mq, mk, mv, og, gc, gr, c0, n0, m0, norm_m_l)


def _merge_kernel(x_ref, sh_ref, sc_ref, gt_ref, gpre_ref, gpost_ref, oa_ref, om_ref,
                  wg_ref, wpa_ref, wpb_ref, wo_ref, o_ref):
    x = x_ref[...]
    h = _mod_norm(x, gpre_ref[...], sc_ref[0], sh_ref[0]).astype(BF16)
    gates = _sigmoid(_dot(h, wg_ref[...]))
    merged = (gates[:, :D_MODEL] * _dot(oa_ref[...], wpa_ref[...])
              + gates[:, D_MODEL:] * _dot(om_ref[...], wpb_ref[...]))
    y = _dot(merged.astype(BF16), wo_ref[...])
    o_ref[...] = x + gt_ref[0] * (_rms(y) * gpost_ref[...])


def _merge(x, sh, sc, gt, g_pre, g_post, oa, om, w_g, w_pa, w_pb, w_out, *, tm, tpg):
    n = x.shape[0]
    r = sh.shape[1]
    row = lambda i: (i, 0)
    const2 = lambda i: (0, 0)
    mod_spec = pl.BlockSpec((1, r, D_MODEL), lambda i: (i // tpg, 0, 0))
    vec_spec = pl.BlockSpec((1, D_MODEL), const2)
    return pl.pallas_call(
        _merge_kernel,
        grid=(n // tm,),
        in_specs=[
            pl.BlockSpec((tm, D_MODEL), row),
            mod_spec, mod_spec, mod_spec, vec_spec, vec_spec,
            pl.BlockSpec((tm, A_WIDTH), row),
            pl.BlockSpec((tm, M_WIDTH), row),
            pl.BlockSpec(w_g.shape, const2),
            pl.BlockSpec(w_pa.shape, const2),
            pl.BlockSpec(w_pb.shape, const2),
            pl.BlockSpec(w_out.shape, const2),
        ],
        out_specs=pl.BlockSpec((tm, D_MODEL), row),
        out_shape=jax.ShapeDtypeStruct((n, D_MODEL), F32),
        compiler_params=_params("arbitrary"),
        name="merge_out_proj",
    )(x, sh, sc, gt, g_pre, g_post, oa, om, w_g, w_pa, w_pb, w_out)


def _prep_weights(w_ffn1, w_ffn3, w_ffn2, w_in, b_if, w_pa, w_pb, w_out):
    gate_cols = jnp.concatenate(
        [w_in[:, :, OFF_MI:OFF_G], w_in[:, :, OFF_MF:OFF_G]], axis=-1)
    w_gate = jnp.pad(gate_cols, ((0, 0), (0, 0), (0, LANES - 3 * H_M))).astype(BF16)
    w_gate_t = jnp.pad(jnp.swapaxes(gate_cols, 1, 2), ((0, 0), (0, GATE_ROWS - 3 * H_M), (0, 0))).astype(BF16)
    b12 = jnp.concatenate([b_if, b_if[:, H_M:]], axis=-1).astype(F32)
    return dict(
        w1=w_ffn1.astype(BF16), w3=w_ffn3.astype(BF16), w2=w_ffn2.astype(BF16),
        w_main=w_in[:, :, :OFF_MI].astype(BF16), w_g=w_in[:, :, OFF_G:].astype(BF16),
        w_gate=w_gate, w_gate_t=w_gate_t,
        b_gate_row=jnp.pad(b12, ((0, 0), (0, LANES - 3 * H_M)))[:, None, :],
        b_gate_col=jnp.pad(b12, ((0, 0), (0, GATE_ROWS - 3 * H_M)))[:, :, None],
        w_pa=w_pa.astype(BF16), w_pb=w_pb.astype(BF16), w_out=w_out.astype(BF16),
    )


def _trunk(x, mods, past, wts, norm_g, conv_w, conv_b, lam, norm_a, norm_m, *, per_token_mod,
           tm, tf, t_attn, tt):
    bsz, seq, _ = x.shape
    n = bsz * seq
    xf = x.reshape(n, D_MODEL)
    depth = norm_g.shape[0]
    chunk = CHUNK if past is None else seq
    new = ([], [], [], [], [], [])
    for l in range(depth):
        if per_token_mod:
            md = [jnp.repeat(mods[l, :, i], seq, axis=0)[None] for i in range(N_MOD)]
            ftm, tpg = n, 1
        else:
            md = [mods[l, :, i][:, None, :] for i in range(N_MOD)]
            ftm, tpg = tm, seq // tm
        md_seq = [mods[l, :, i][:, None, :] for i in range(N_MOD)]
        ng = norm_g[l][:, None, :]
        lam_init = 0.8 - 0.6 * math.exp(-0.3 * l)

        xf = _ffn(xf, md[0], md[1], md[2], ng[0], ng[1], wts["w1"][l, 0], wts["w3"][l, 0],
                  wts["w2"][l, 0], tm=ftm, tf=tf, tpg=tpg)

        if past is None:
            c0 = jnp.zeros((bsz, H_M, DH_M, DH_M), F32)
            n0 = jnp.zeros((bsz, H_M, 1, DH_M), F32)
            m0 = jnp.zeros((bsz, H_M, 1, LANES), F32)
            conv0 = jnp.zeros((bsz, SUBLANES, 2 * M_WIDTH), F32)
        else:
            past_k, past_v, c0, n0, m0, conv_state = (a[l] for a in past)
            n0 = n0[:, :, None, :]
            m0 = jnp.broadcast_to(m0[:, :, None, None], (bsz, H_M, 1, LANES))
            conv0 = jnp.pad(conv_state, ((0, 0), (SUBLANES - (CONV_W - 1), 0), (0, 0)))
        itm = min(tm, seq)
        (q, k32, v32, kb, vb, mq, mk, mv, og, gc, gr, tail) = _inproj(
            xf, md_seq[3], md_seq[4], ng[2], wts["w_main"][l], wts["w_gate"][l], wts["w_gate_t"][l],
            wts["b_gate_row"][l], wts["b_gate_col"][l], conv_w[l], conv_b[l][None, :], conv0,
            tm=itm, tps=seq // itm, chunk=chunk)
        gr = gr.reshape(n // itm, GATE_ROWS, itm // chunk, chunk)
        gr = jnp.swapaxes(gr, 1, 2).reshape(n // chunk, GATE_ROWS, chunk)

        if past is None:
            oa = _attn_prompt(q, kb, vb, lam[l], norm_a[l][None, :], bsz=bsz, s=seq, t=t_attn,
                              lam_init=lam_init)
        else:
            oa = _attn_sample(q, k32, v32, past_k, past_v, lam[l], norm_a[l][None, :], bsz=bsz,
                              t=seq, lam_init=lam_init)
        om, c_new, n_new, m_new = _mlstm(mq, mk, mv, og, gc, gr, c0, n0, m0, norm_m[l][None, :],
                                         bsz=bsz, seq=seq, tt=min(tt, seq), chunk=chunk)

        xf = _merge(xf, md[3], md[4], md[5], ng[2], ng[3], oa, om, wts["w_g"][l], wts["w_pa"][l],
                    wts["w_pb"][l], wts["w_out"][l], tm=ftm, tpg=tpg)
        xf = _ffn(xf, md[6], md[7], md[8], ng[4], ng[5], wts["w1"][l, 1], wts["w3"][l, 1],
                  wts["w2"][l, 1], tm=ftm, tf=tf, tpg=tpg)

        states = (k32.reshape(bsz, seq, H_A, 2 * DH_A), v32.reshape(bsz, seq, H_A, 2 * DH_A),
                  c_new, n_new[:, :, 0, :], m_new[:, :, 0, 0], tail[:, SUBLANES - (CONV_W - 1):, :])
        for lst, s in zip(new, states):
            lst.append(s)
    return xf.reshape(bsz, seq, D_MODEL), [jnp.stack(s) for s in new]


def _forward(x_prompt, x_sample, c_prompt, c_sample, cache_k, cache_v, state_c, state_n, state_m,
             state_conv, w_ada, b_ada, norm_g, w_ffn1, w_ffn3, w_ffn2, w_in, b_if, conv_w, conv_b,
             lam, norm_a, norm_m, w_pa, w_pb, w_out, *, tm, tf, t_attn, tt):
    bp = x_prompt.shape[0]
    depth = w_ada.shape[0]
    mods = _ada(jnp.concatenate([c_prompt, c_sample], axis=0), w_ada, b_ada)
    mods = mods.reshape(depth, -1, N_MOD, D_MODEL)
    wts = _prep_weights(w_ffn1, w_ffn3, w_ffn2, w_in, b_if, w_pa, w_pb, w_out)
    common = (wts, norm_g, conv_w, conv_b, lam, norm_a, norm_m)
    seq = x_prompt.shape[1]
    y_p, st_p = _trunk(x_prompt, mods[:, :bp], None, *common, per_token_mod=False,
                       tm=min(tm, seq), tf=tf, t_attn=min(t_attn, seq), tt=tt)
    y_s, st_s = _trunk(x_sample, mods[:, bp:], (cache_k, cache_v, state_c, state_n, state_m, state_conv),
                       *common, per_token_mod=True, tm=tm, tf=tf, t_attn=t_attn, tt=tt)
    return (y_p, y_s, *st_p, *st_s)


def kernel(x_prompt, x_sample, c_prompt, c_sample, cache_k, cache_v, state_c, state_n, state_m, state_conv, w_ada, b_ada, norm_g, w_ffn1, w_ffn3, w_ffn2, w_in, b_if, conv_w, conv_b, lam, norm_a, norm_m, w_pa, w_pb, w_out):
    return _forward(x_prompt, x_sample, c_prompt, c_sample, cache_k, cache_v, state_c, state_n, state_m,
                    state_conv, w_ada, b_ada, norm_g, w_ffn1, w_ffn3, w_ffn2, w_in, b_if, conv_w, conv_b,
                    lam, norm_a, norm_m, w_pa, w_pb, w_out, tm=512, tf=1408, t_attn=512, tt=512)
```
